```python
import jax, jax.numpy as jnp
from jax import lax
import numpy as np

D_MODEL = 1024
BATCH = 32
SEQ = 2048
DEPTH = 4

HEAD_DIM = 64
D_MIX = D_MODEL
D_ATTN = D_MIX // 2
D_POOL = D_MIX // 4
D_CONV = D_MIX // 4
N_ATTN_HEADS = D_ATTN // HEAD_DIM
POOL_WINDOWS = (2, 4, 8, 16)
N_POOL_GROUPS = len(POOL_WINDOWS)
POOL_GROUP_DIM = D_POOL // N_POOL_GROUPS
N_CONV_HEADS = D_CONV // HEAD_DIM
CONV_WIDTH = 3
D_FF = 2816
Q_BLOCK = 128
RMS_EPS = 1e-6
IN_SPLITS = (D_ATTN, D_ATTN, D_ATTN, N_ATTN_HEADS, D_POOL, D_CONV, D_CONV, D_CONV)
D_IN = sum(IN_SPLITS)

kernel_name = "hymba_style_fox_pool_shortconv_macaron"


def _rmsnorm(x, g):
    x32 = x.astype(jnp.float32)
    y = x32 * lax.rsqrt(jnp.mean(x32 * x32, axis=-1, keepdims=True) + RMS_EPS)
    return (y * g.astype(jnp.float32)).astype(x.dtype)


def _swiglu(h, w_in, w_out):
    gate, up = jnp.split(h @ w_in, 2, axis=-1)
    return (jax.nn.silu(gate) * up) @ w_out


def _fox_attention(q, k, v, f_logit, b_forget):
    b, s, _ = q.shape
    qh = q.reshape(b, s, N_ATTN_HEADS, HEAD_DIM).transpose(0, 2, 1, 3)
    kh = k.reshape(b, s, N_ATTN_HEADS, HEAD_DIM).transpose(0, 2, 1, 3)
    vh = v.reshape(b, s, N_ATTN_HEADS, HEAD_DIM).transpose(0, 2, 1, 3)
    log_f = jax.nn.log_sigmoid(f_logit.astype(jnp.float32) + b_forget.astype(jnp.float32))
    dcum = lax.cumsum(log_f, axis=1).transpose(0, 2, 1)
    scale = HEAD_DIM ** -0.5
    outs = []
    for i in range(s // Q_BLOCK):
        q0 = i * Q_BLOCK
        end = q0 + Q_BLOCK
        sc = jnp.einsum('bhqd,bhkd->bhqk', qh[:, :, q0:end], kh[:, :, :end],
                        preferred_element_type=jnp.float32) * scale
        sc = sc + dcum[:, :, q0:end, None] - dcum[:, :, None, :end]
        mask = jnp.arange(end)[None, :] <= (q0 + jnp.arange(Q_BLOCK))[:, None]
        sc = jnp.where(mask, sc, -jnp.inf)
        p = jax.nn.softmax(sc, axis=-1)
        outs.append(jnp.einsum('bhqk,bhkd->bhqd', p.astype(vh.dtype), vh[:, :, :end]))
    o = jnp.concatenate(outs, axis=2)
    return o.transpose(0, 2, 1, 3).reshape(b, s, D_ATTN)


def _multiscale_pool(u, w_pool, pool_scale):
    b, s, _ = u.shape
    groups = jnp.split(u, N_POOL_GROUPS, axis=-1)
    pos = jnp.arange(s)
    outs = []
    for gi, w in enumerate(POOL_WINDOWS):
        ug = groups[gi]
        cs = lax.cumsum(ug.astype(jnp.float32), axis=1)
        cs_pad = jnp.pad(cs, ((0, 0), (w, 0), (0, 0)))
        win_sum = cs - cs_pad[:, :s]
        count = jnp.minimum(pos + 1, w).astype(jnp.float32)[None, :, None]
        pooled = (win_sum / count).astype(ug.dtype) - ug
        outs.append(jnp.einsum('bsc,cd->bsd', pooled, w_pool[gi]))
    return jnp.concatenate(outs, axis=-1) * pool_scale


def _short_conv(b_gate, c_gate, h, conv_w):
    u = c_gate * h
    rhs = conv_w.reshape(CONV_WIDTH, 1, D_CONV).astype(u.dtype)
    y = lax.conv_general_dilated(u, rhs, window_strides=(1,), padding=((CONV_WIDTH - 1, 0),),
                                 dimension_numbers=('NWC', 'WIO', 'NWC'),
                                 feature_group_count=D_CONV)
    return b_gate * y


def _hybrid_mixer(xn, w_in, b_forget, w_pool, pool_scale, conv_w, w_out):
    proj = xn @ w_in
    idx = list(np.cumsum(IN_SPLITS)[:-1])
    q, k, v, f_logit, pool_in, cb, cc, ch = jnp.split(proj, idx, axis=-1)
    y_attn = _fox_attention(q, k, v, f_logit, b_forget)
    y_pool = _multiscale_pool(pool_in, w_pool, pool_scale)
    y_conv = _short_conv(cb, cc, ch, conv_w)
    return jnp.concatenate([y_attn, y_pool, y_conv], axis=-1) @ w_out


def setup_inputs(seed: int = 0) -> dict:
    key = jax.random.key(seed)
    ks = jax.random.split(key, 16)
    f32 = jnp.float32

    def nrm(k, shape, fan_in):
        return jax.random.normal(k, shape, f32) * (fan_in ** -0.5)

    def gain(k, shape):
        return 1.0 + 0.1 * jax.random.normal(k, shape, f32)

    return {
        "x": jax.random.normal(ks[0], (BATCH, SEQ, D_MODEL), f32),
        "norm_ffn1": gain(ks[1], (DEPTH, D_MODEL)),
        "w_ffn1_in": nrm(ks[2], (DEPTH, D_MODEL, 2 * D_FF), D_MODEL),
        "w_ffn1_out": nrm(ks[3], (DEPTH, D_FF, D_MODEL), D_FF),
        "norm_mix": gain(ks[4], (DEPTH, D_MODEL)),
        "w_mix_in": nrm(ks[5], (DEPTH, D_MODEL, D_IN), D_MODEL),
        "b_forget": 3.0 + 0.5 * jax.random.normal(ks[6], (DEPTH, N_ATTN_HEADS), f32),
        "w_pool": nrm(ks[7], (DEPTH, N_POOL_GROUPS, POOL_GROUP_DIM, POOL_GROUP_DIM), POOL_GROUP_DIM),
        "pool_scale": gain(ks[8], (DEPTH, D_POOL)),
        "conv_w": nrm(ks[9], (DEPTH, CONV_WIDTH, D_CONV), CONV_WIDTH),
        "w_mix_out": nrm(ks[10], (DEPTH, D_MIX, D_MODEL), D_MIX),
        "norm_ffn2": gain(ks[11], (DEPTH, D_MODEL)),
        "w_ffn2_in": nrm(ks[12], (DEPTH, D_MODEL, 2 * D_FF), D_MODEL),
        "w_ffn2_out": nrm(ks[13], (DEPTH, D_FF, D_MODEL), D_FF),
        "norm_final": gain(ks[14], (D_MODEL,)),
    }


def reference(x, norm_ffn1, w_ffn1_in, w_ffn1_out, norm_mix, w_mix_in, b_forget, w_pool,
              pool_scale, conv_w, w_mix_out, norm_ffn2, w_ffn2_in, w_ffn2_out, norm_final):
    for l in range(DEPTH):
        x = x + 0.5 * _swiglu(_rmsnorm(x, norm_ffn1[l]), w_ffn1_in[l], w_ffn1_out[l])
        x = x + _hybrid_mixer(_rmsnorm(x, norm_mix[l]), w_mix_in[l], b_forget[l], w_pool[l],
                              pool_scale[l], conv_w[l], w_mix_out[l])
        x = x + 0.5 * _swiglu(_rmsnorm(x, norm_ffn2[l]), w_ffn2_in[l], w_ffn2_out[l])
    return _rmsnorm(x, norm_final)
```

```python
import functools

import numpy as np
import jax
import jax.numpy as jnp
from jax import lax
from jax.experimental import pallas as pl
from jax.experimental.pallas import tpu as pltpu

F32 = jnp.float32
BF16 = jnp.bfloat16

D_MODEL = 1024
SEQ = 2048
HEAD_DIM = 64
N_HEADS = 8
D_ATTN = N_HEADS * HEAD_DIM
D_POOL = 256
D_CONV = 256
POOL_WINDOWS = (2, 4, 8, 16)
POOL_GROUP_DIM = D_POOL // len(POOL_WINDOWS)
CONV_WIDTH = 3
D_FF = 2816
D_IN = 3 * D_ATTN + N_HEADS + D_POOL + 3 * D_CONV
RMS_EPS = 1e-6
ATTN_SCALE = HEAD_DIM ** -0.5

LANES = 128
SUBLANES = 8
VMEM_LIMIT_BYTES = 56 * 1024 * 1024

FFN_ROWS = 512
FF_CHUNKS = ((0, 1024), (1024, 2048), (2048, D_FF))
MIX_ROWS = 512
OUT_ROWS = 1024
NORM_ROWS = 1024
ATT_BLOCK = 256
PAIR = 2 * HEAD_DIM
N_PAIRS = N_HEADS // 2
GATE_SPLIT = 3
POOL_HALO = max(POOL_WINDOWS)
CONV_HALO = SUBLANES
ONES_ROWS = 16
MASKED = -1e30

C_Q = 0
C_K = C_Q + D_ATTN
C_V = C_K + D_ATTN
C_F = C_V + D_ATTN
C_POOL = C_F + LANES
C_CB = C_POOL + D_POOL
C_CC = C_CB + D_CONV
C_CH = C_CC + D_CONV
D_PROJ = C_CH + D_CONV


def _rmsnorm(x, g):
    return x * lax.rsqrt(jnp.mean(x * x, axis=-1, keepdims=True) + RMS_EPS) * g


def _compiler_params(n_axes):
    return pltpu.CompilerParams(
        dimension_semantics=("arbitrary",) * n_axes,
        vmem_limit_bytes=VMEM_LIMIT_BYTES,
    )


def _resident(shape, layer):
    nd = len(shape)
    return pl.BlockSpec((None,) + tuple(shape), lambda *_: (layer,) + (0,) * nd,
                        pipeline_mode=pl.Buffered(1))


def _ffn_kernel(x_ref, g_ref, win_ref, wout_ref, o_ref):
    x = x_ref[...]
    h = _rmsnorm(x, g_ref[...]).astype(BF16)
    acc = None
    for c0, c1 in FF_CHUNKS:
        gate = jnp.dot(h, win_ref[:, c0:c1], preferred_element_type=F32)
        up = jnp.dot(h, win_ref[:, D_FF + c0:D_FF + c1], preferred_element_type=F32)
        act = (gate * jax.nn.sigmoid(gate) * up).astype(BF16)
        part = jnp.dot(act, wout_ref[c0:c1, :], preferred_element_type=F32)
        acc = part if acc is None else acc + part
    o_ref[...] = x + 0.5 * acc


def _ffn(x, g, w_in, w_out, layer):
    n = x.shape[0]
    return pl.pallas_call(
        _ffn_kernel,
        grid=(n // FFN_ROWS,),
        in_specs=[
            pl.BlockSpec((FFN_ROWS, D_MODEL), lambda i: (i, 0)),
            _resident((1, D_MODEL), layer),
            _resident((D_MODEL, 2 * D_FF), layer),
            _resident((D_FF, D_MODEL), layer),
        ],
        out_specs=pl.BlockSpec((FFN_ROWS, D_MODEL), lambda i: (i, 0)),
        out_shape=jax.ShapeDtypeStruct((n, D_MODEL), F32),
        compiler_params=_compiler_params(1),
        name="ffn",
    )(x, g, w_in, w_out)


def _mix_in_kernel(x_ref, g_ref, w_ref, bf_ref, sel_ref, wpool_ref, pscale_ref, convw_ref,
                   q_ref, kx_ref, v_ref, ypc_ref, dcarry, pcarry, ccarry):
    tiles_per_seq = SEQ // MIX_ROWS
    t_in_seq = lax.rem(pl.program_id(0), tiles_per_seq)

    @pl.when(t_in_seq == 0)
    def _():
        dcarry[...] = jnp.zeros_like(dcarry)
        pcarry[...] = jnp.zeros_like(pcarry)
        ccarry[...] = jnp.zeros_like(ccarry)

    xn = _rmsnorm(x_ref[...], g_ref[...]).astype(BF16)
    proj = jnp.dot(xn, w_ref[...], preferred_element_type=F32)

    q_ref[...] = (proj[:, C_Q:C_K] * ATTN_SCALE).astype(BF16)
    v_ref[...] = proj[:, C_V:C_F].astype(BF16)

    f = proj[:, C_F:C_POOL] + bf_ref[...]
    logf = jnp.minimum(f, 0.0) - jnp.log1p(jnp.exp(-jnp.abs(f)))
    rows = lax.broadcasted_iota(jnp.int32, logf.shape, 0)
    csum = logf
    shift = 1
    while shift < MIX_ROWS:
        csum = csum + jnp.where(rows >= shift, pltpu.roll(csum, shift, axis=0), 0.0)
        shift *= 2
    dcum = csum + dcarry[0:1, :]
    dcarry[...] = jnp.broadcast_to(dcum[MIX_ROWS - 1:MIX_ROWS, :], dcarry.shape)
    hi = dcum.astype(BF16).astype(F32)
    r1 = dcum - hi
    mid = r1.astype(BF16).astype(F32)
    lo = r1 - mid
    aug = -(hi * sel_ref[0:1, :] + mid * sel_ref[1:2, :] + lo * sel_ref[2:3, :])
    aug = aug.astype(BF16)
    for p in range(N_PAIRS):
        kx_ref[:, 2 * PAIR * p:2 * PAIR * p + PAIR] = (
            proj[:, C_K + PAIR * p:C_K + PAIR * (p + 1)].astype(BF16))
        kx_ref[:, 2 * PAIR * p + PAIR:2 * PAIR * (p + 1)] = aug

    u = proj[:, C_POOL:C_CB]
    ext = jnp.concatenate([pcarry[...], u], axis=0)
    s2 = ext + pltpu.roll(ext, 1, axis=0)
    s4 = s2 + pltpu.roll(s2, 2, axis=0)
    s8 = s4 + pltpu.roll(s4, 4, axis=0)
    s16 = s8 + pltpu.roll(s8, 8, axis=0)
    lane = lax.broadcasted_iota(jnp.int32, (MIX_ROWS, D_POOL), 1)
    group = lax.shift_right_logical(lane, POOL_GROUP_DIM.bit_length() - 1)
    win_sum = jnp.where(group == 0, s2[POOL_HALO:], jnp.where(
        group == 1, s4[POOL_HALO:], jnp.where(group == 2, s8[POOL_HALO:], s16[POOL_HALO:])))
    window = jnp.where(group == 0, POOL_WINDOWS[0], jnp.where(
        group == 1, POOL_WINDOWS[1], jnp.where(group == 2, POOL_WINDOWS[2], POOL_WINDOWS[3])))
    pos = t_in_seq * MIX_ROWS + lax.broadcasted_iota(jnp.int32, (MIX_ROWS, D_POOL), 0)
    count = jnp.minimum(pos + 1, window).astype(F32)
    pooled = (win_sum / count - u).astype(BF16)
    y_pool = jnp.dot(pooled, wpool_ref[...], preferred_element_type=F32) * pscale_ref[...]
    pcarry[...] = u[MIX_ROWS - POOL_HALO:, :]

    cu = proj[:, C_CC:C_CH] * proj[:, C_CH:D_PROJ]
    cext = jnp.concatenate([ccarry[...], cu], axis=0)
    conv = (convw_ref[2:3, :] * cext + convw_ref[1:2, :] * pltpu.roll(cext, 1, axis=0)
            + convw_ref[0:1, :] * pltpu.roll(cext, 2, axis=0))
    y_conv = proj[:, C_CB:C_CC] * conv[CONV_HALO:]
    ccarry[...] = cu[MIX_ROWS - CONV_HALO:, :]

    ypc_ref[:, 0:D_POOL] = y_pool.astype(BF16)
    ypc_ref[:, D_POOL:D_POOL + D_CONV] = y_conv.astype(BF16)


def _mix_in(x, g, w, bf, sel, wpool, pscale, convw, layer):
    n = x.shape[0]
    rows = lambda width: pl.BlockSpec((MIX_ROWS, width), lambda i: (i, 0))
    return pl.pallas_call(
        _mix_in_kernel,
        grid=(n // MIX_ROWS,),
        in_specs=[
            rows(D_MODEL),
            _resident((1, D_MODEL), layer),
            _resident((D_MODEL, D_PROJ), layer),
            _resident((1, LANES), layer),
            pl.BlockSpec((SUBLANES, LANES), lambda i: (0, 0)),
            _resident((D_POOL, D_POOL), layer),
            _resident((1, D_POOL), layer),
            _resident((CONV_WIDTH, D_CONV), layer),
        ],
        out_specs=[rows(D_ATTN), rows(2 * D_ATTN), rows(D_ATTN), rows(D_POOL + D_CONV)],
        out_shape=[
            jax.ShapeDtypeStruct((n, D_ATTN), BF16),
            jax.ShapeDtypeStruct((n, 2 * D_ATTN), BF16),
            jax.ShapeDtypeStruct((n, D_ATTN), BF16),
            jax.ShapeDtypeStruct((n, D_POOL + D_CONV), BF16),
        ],
        scratch_shapes=[
            pltpu.VMEM((SUBLANES, LANES), F32),
            pltpu.VMEM((POOL_HALO, D_POOL), F32),
            pltpu.VMEM((CONV_HALO, D_CONV), F32),
        ],
        compiler_params=_compiler_params(1),
        name="mix_in",
    )(x, g, w, bf, sel, wpool, pscale, convw)


def _attn_kernel(q_ref, kx_ref, v_ref, o_ref, vt_ref):
    pair = pl.program_id(1)
    vt = jnp.transpose(v_ref[...].astype(F32)).astype(BF16)
    ones = jnp.ones((ONES_ROWS, SEQ), BF16)
    for hd in range(2):
        vt_ref[hd, 0:HEAD_DIM, :] = vt[hd * HEAD_DIM:(hd + 1) * HEAD_DIM, :]
        vt_ref[hd, HEAD_DIM:HEAD_DIM + ONES_ROWS, :] = ones

    row = lax.broadcasted_iota(jnp.int32, (LANES, ATT_BLOCK), 0)
    krow = lax.broadcasted_iota(jnp.int32, (ATT_BLOCK, ATT_BLOCK), 0)
    qcol = lax.broadcasted_iota(jnp.int32, (ATT_BLOCK, ATT_BLOCK), 1)
    causal = krow <= qcol
    zeros_half = jnp.zeros((HEAD_DIM, ATT_BLOCK), BF16)

    for qi in range(SEQ // ATT_BLOCK):
        q0 = qi * ATT_BLOCK
        qt = jnp.transpose(q_ref[q0:q0 + ATT_BLOCK, :].astype(F32)).astype(BF16)
        o_heads = []
        for hd in range(2):
            gate_row = GATE_SPLIT * (2 * pair + hd)
            pick = jnp.where((row >= gate_row) & (row < gate_row + GATE_SPLIT), 1.0, 0.0).astype(BF16)
            q_half = qt[hd * HEAD_DIM:(hd + 1) * HEAD_DIM, :]
            halves = [q_half, zeros_half] if hd == 0 else [zeros_half, q_half]
            w = jnp.concatenate(halves + [pick], axis=0)
            m = jnp.full((1, ATT_BLOCK), MASKED, F32)
            acc = jnp.zeros((HEAD_DIM + ONES_ROWS, ATT_BLOCK), F32)
            for kj in range(qi + 1):
                k0 = kj * ATT_BLOCK
                st = jnp.dot(kx_ref[k0:k0 + ATT_BLOCK, :], w, preferred_element_type=F32)
                if kj == qi:
                    st = jnp.where(causal, st, MASKED)
                m_new = jnp.maximum(m, jnp.max(st, axis=0, keepdims=True))
                alpha = jnp.exp(m - m_new)
                pt = jnp.exp(st - m_new).astype(BF16)
                acc = acc * alpha + jnp.dot(vt_ref[hd, :, k0:k0 + ATT_BLOCK], pt,
                                            preferred_element_type=F32)
                m = m_new
            o_heads.append(acc[0:HEAD_DIM, :] / acc[HEAD_DIM:HEAD_DIM + 1, :])
        ot = jnp.concatenate(o_heads, axis=0)
        o_ref[q0:q0 + ATT_BLOCK, :] = jnp.transpose(ot).astype(BF16)


def _attn(q, kx, v):
    n = q.shape[0]
    batch = n // SEQ
    return pl.pallas_call(
        _attn_kernel,
        grid=(batch, N_PAIRS),
        in_specs=[
            pl.BlockSpec((SEQ, PAIR), lambda b, p: (b, p)),
            pl.BlockSpec((SEQ, 2 * PAIR), lambda b, p: (b, p)),
            pl.BlockSpec((SEQ, PAIR), lambda b, p: (b, p)),
        ],
        out_specs=pl.BlockSpec((SEQ, PAIR), lambda b, p: (b, p)),
        out_shape=jax.ShapeDtypeStruct((n, D_ATTN), BF16),
        scratch_shapes=[pltpu.VMEM((2, HEAD_DIM + ONES_ROWS, SEQ), BF16)],
        compiler_params=_compiler_params(2),
        name="attn",
    )(q, kx, v)


def _mix_out_kernel(x_ref, ya_ref, ypc_ref, w_ref, o_ref):
    y = jnp.dot(ya_ref[...], w_ref[0:D_ATTN, :], preferred_element_type=F32)
    y = y + jnp.dot(ypc_ref[...], w_ref[D_ATTN:, :], preferred_element_type=F32)
    o_ref[...] = x_ref[...] + y


def _mix_out(x, ya, ypc, w, layer):
    n = x.shape[0]
    rows = lambda width: pl.BlockSpec((OUT_ROWS, width), lambda i: (i, 0))
    return pl.pallas_call(
        _mix_out_kernel,
        grid=(n // OUT_ROWS,),
        in_specs=[rows(D_MODEL), rows(D_ATTN), rows(D_POOL + D_CONV),
                  _resident((D_MODEL, D_MODEL), layer)],
        out_specs=rows(D_MODEL),
        out_shape=jax.ShapeDtypeStruct((n, D_MODEL), F32),
        compiler_params=_compiler_params(1),
        name="mix_out",
    )(x, ya, ypc, w)


def _norm_kernel(x_ref, g_ref, o_ref):
    o_ref[...] = _rmsnorm(x_ref[...], g_ref[...])


def _final_norm(x, g):
    n = x.shape[0]
    return pl.pallas_call(
        _norm_kernel,
        grid=(n // NORM_ROWS,),
        in_specs=[pl.BlockSpec((NORM_ROWS, D_MODEL), lambda i: (i, 0)),
                  pl.BlockSpec((1, D_MODEL), lambda i: (0, 0))],
        out_specs=pl.BlockSpec((NORM_ROWS, D_MODEL), lambda i: (i, 0)),
        out_shape=jax.ShapeDtypeStruct((n, D_MODEL), F32),
        compiler_params=_compiler_params(1),
        name="final_norm",
    )(x, g)


def _gate_lanes(a):
    rep = jnp.repeat(a, GATE_SPLIT, axis=-1)
    pad = [(0, 0)] * (a.ndim - 1) + [(0, LANES - GATE_SPLIT * N_HEADS)]
    return jnp.pad(rep, pad)


def _split_selector():
    sel = np.zeros((SUBLANES, LANES), np.float32)
    for lane in range(GATE_SPLIT * N_HEADS):
        sel[lane % GATE_SPLIT, lane] = 1.0
    return jnp.asarray(sel)


def _mix_in_weight(w_mix_in):
    o_f = 3 * D_ATTN
    o_pool = o_f + N_HEADS
    o_conv = o_pool + D_POOL
    qkv = w_mix_in[..., :o_f]
    gates = _gate_lanes(w_mix_in[..., o_f:o_pool])
    rest = w_mix_in[..., o_pool:]
    assert rest.shape[-1] == D_POOL + 3 * D_CONV and o_conv + 3 * D_CONV == D_IN
    return jnp.concatenate([qkv, gates, rest], axis=-1).astype(BF16)


def _pool_weight(w_pool):
    depth, groups = w_pool.shape[:2]
    eye = jnp.eye(groups, dtype=w_pool.dtype)
    bd = jnp.einsum("lgij,gh->lgihj", w_pool, eye)
    return bd.reshape(depth, D_POOL, D_POOL).astype(BF16)


def kernel(x, norm_ffn1, w_ffn1_in, w_ffn1_out, norm_mix, w_mix_in, b_forget, w_pool, pool_scale,
           conv_w, w_mix_out, norm_ffn2, w_ffn2_in, w_ffn2_out, norm_final):
    batch, seq, d_model = x.shape
    depth = norm_ffn1.shape[0]
    assert (seq, d_model) == (SEQ, D_MODEL) and w_mix_in.shape[-1] == D_IN
    assert w_ffn1_in.shape[1:] == (D_MODEL, 2 * D_FF)

    row3 = lambda a: a.reshape(depth, 1, a.shape[-1])
    g1, gm, g2 = row3(norm_ffn1), row3(norm_mix), row3(norm_ffn2)
    w1i, w1o = w_ffn1_in.astype(BF16), w_ffn1_out.astype(BF16)
    w2i, w2o = w_ffn2_in.astype(BF16), w_ffn2_out.astype(BF16)
    wmi = _mix_in_weight(w_mix_in)
    wmo = w_mix_out.astype(BF16)
    bf = row3(_gate_lanes(b_forget))
    sel = _split_selector()
    wpool = _pool_weight(w_pool)
    pscale = row3(pool_scale)

    h = x.reshape(batch * seq, d_model)
    for layer in range(depth):
        h = _ffn(h, g1, w1i, w1o, layer)
        q, kx, v, ypc = _mix_in(h, gm, wmi, bf, sel, wpool, pscale, conv_w, layer)
        ya = _attn(q, kx, v)
        h = _mix_out(h, ya, ypc, wmo, layer)
        h = _ffn(h, g2, w2i, w2o, layer)
    out = _final_norm(h, norm_final.reshape(1, d_model))
    return out.reshape(batch, seq, d_model)
```

```python
import functools

import numpy as np
import jax
import jax.numpy as jnp
from jax import lax
from jax.experimental import pallas as pl
from jax.experimental.pallas import tpu as pltpu

F32 = jnp.float32
BF16 = jnp.bfloat16

D_MODEL = 1024
SEQ = 2048
HEAD_DIM = 64
N_HEADS = 8
D_ATTN = N_HEADS * HEAD_DIM
D_POOL = 256
D_CONV = 256
POOL_WINDOWS = (2, 4, 8, 16)
POOL_GROUP_DIM = D_POOL // len(POOL_WINDOWS)
CONV_WIDTH = 3
D_FF = 2816
D_IN = 3 * D_ATTN + N_HEADS + D_POOL + 3 * D_CONV
RMS_EPS = 1e-6
ATTN_SCALE = HEAD_DIM ** -0.5
LOG2_E = 1.4426950408889634

LANES = 128
SUBLANES = 8
VMEM_LIMIT_BYTES = 56 * 1024 * 1024

FFN_ROWS = 512
FF_CHUNKS = ((0, 1024), (1024, 2048), (2048, D_FF))
MIX_ROWS = 512
OUT_ROWS = 1024
NORM_ROWS = 1024
ATT_BLOCK = 256
QUERY_BLOCK_ORDER = tuple(range(SEQ // ATT_BLOCK))
SCORE_LOOKAHEAD = 2
PAIR = 2 * HEAD_DIM
N_PAIRS = N_HEADS // 2
GATE_SPLIT = 3
POOL_HALO = max(POOL_WINDOWS)
CONV_HALO = SUBLANES
ONES_ROWS = 16
MASKED = -1e30

C_Q = 0
C_K = C_Q + D_ATTN
C_V = C_K + D_ATTN
C_F = C_V + D_ATTN
C_POOL = C_F + LANES
C_CB = C_POOL + D_POOL
C_CC = C_CB + D_CONV
C_CH = C_CC + D_CONV
D_PROJ = C_CH + D_CONV


def _rmsnorm(x, g):
    return x * lax.rsqrt(jnp.mean(x * x, axis=-1, keepdims=True) + RMS_EPS) * g


def _compiler_params(n_axes, flags=None):
    return pltpu.CompilerParams(
        dimension_semantics=("arbitrary",) * n_axes,
        vmem_limit_bytes=VMEM_LIMIT_BYTES,
        flags=flags,
    )


def _resident(shape, layer):
    nd = len(shape)
    return pl.BlockSpec((None,) + tuple(shape), lambda *_: (layer,) + (0,) * nd,
                        pipeline_mode=pl.Buffered(1))


def _ffn_kernel(x_ref, g_ref, win_ref, wout_ref, o_ref):
    x = x_ref[...]
    h = _rmsnorm(x, g_ref[...]).astype(BF16)
    acc = None
    for c0, c1 in FF_CHUNKS:
        gate = jnp.dot(h, win_ref[:, c0:c1], preferred_element_type=F32)
        up = jnp.dot(h, win_ref[:, D_FF + c0:D_FF + c1], preferred_element_type=F32)
        act = (gate * jax.nn.sigmoid(gate) * up).astype(BF16)
        part = jnp.dot(act, wout_ref[c0:c1, :], preferred_element_type=F32)
        acc = part if acc is None else acc + part
    o_ref[...] = x + 0.5 * acc


def _ffn(x, g, w_in, w_out, layer):
    n = x.shape[0]
    return pl.pallas_call(
        _ffn_kernel,
        grid=(n // FFN_ROWS,),
        in_specs=[
            pl.BlockSpec((FFN_ROWS, D_MODEL), lambda i: (i, 0)),
            _resident((1, D_MODEL), layer),
            _resident((D_MODEL, 2 * D_FF), layer),
            _resident((D_FF, D_MODEL), layer),
        ],
        out_specs=pl.BlockSpec((FFN_ROWS, D_MODEL), lambda i: (i, 0)),
        out_shape=jax.ShapeDtypeStruct((n, D_MODEL), F32),
        compiler_params=_compiler_params(1),
        name="ffn",
    )(x, g, w_in, w_out)


def _mix_in_kernel(x_ref, g_ref, w_ref, bf_ref, sel_ref, wpool_ref, pscale_ref, convw_ref,
                   q_ref, kx_ref, v_ref, ypc_ref, dcarry, pcarry, ccarry):
    tiles_per_seq = SEQ // MIX_ROWS
    t_in_seq = lax.rem(pl.program_id(0), tiles_per_seq)

    @pl.when(t_in_seq == 0)
    def _():
        dcarry[...] = jnp.zeros_like(dcarry)
        pcarry[...] = jnp.zeros_like(pcarry)
        ccarry[...] = jnp.zeros_like(ccarry)

    xn = _rmsnorm(x_ref[...], g_ref[...]).astype(BF16)
    proj = jnp.dot(xn, w_ref[...], preferred_element_type=F32)

    q_ref[...] = (proj[:, C_Q:C_K] * (ATTN_SCALE * LOG2_E)).astype(BF16)
    v_ref[...] = proj[:, C_V:C_F].astype(BF16)

    f = proj[:, C_F:C_POOL] + bf_ref[...]
    logf = jnp.minimum(f, 0.0) - jnp.log1p(jnp.exp(-jnp.abs(f)))
    rows = lax.broadcasted_iota(jnp.int32, logf.shape, 0)
    csum = logf
    shift = 1
    while shift < MIX_ROWS:
        csum = csum + jnp.where(rows >= shift, pltpu.roll(csum, shift, axis=0), 0.0)
        shift *= 2
    dcum = csum + dcarry[0:1, :]
    dcarry[...] = jnp.broadcast_to(dcum[MIX_ROWS - 1:MIX_ROWS, :], dcarry.shape)
    d2 = dcum * LOG2_E
    hi = d2.astype(BF16).astype(F32)
    r1 = d2 - hi
    mid = r1.astype(BF16).astype(F32)
    lo = r1 - mid
    aug = -(hi * sel_ref[0:1, :] + mid * sel_ref[1:2, :] + lo * sel_ref[2:3, :])
    aug = aug.astype(BF16)
    for p in range(N_PAIRS):
        kx_ref[:, 2 * PAIR * p:2 * PAIR * p + PAIR] = (
            proj[:, C_K + PAIR * p:C_K + PAIR * (p + 1)].astype(BF16))
        kx_ref[:, 2 * PAIR * p + PAIR:2 * PAIR * (p + 1)] = aug

    u = proj[:, C_POOL:C_CB]
    ext = jnp.concatenate([pcarry[...], u], axis=0)
    s2 = ext + pltpu.roll(ext, 1, axis=0)
    s4 = s2 + pltpu.roll(s2, 2, axis=0)
    s8 = s4 + pltpu.roll(s4, 4, axis=0)
    s16 = s8 + pltpu.roll(s8, 8, axis=0)
    lane = lax.broadcasted_iota(jnp.int32, (MIX_ROWS, D_POOL), 1)
    group = lax.shift_right_logical(lane, POOL_GROUP_DIM.bit_length() - 1)
    win_sum = jnp.where(group == 0, s2[POOL_HALO:], jnp.where(
        group == 1, s4[POOL_HALO:], jnp.where(group == 2, s8[POOL_HALO:], s16[POOL_HALO:])))
    window = jnp.where(group == 0, POOL_WINDOWS[0], jnp.where(
        group == 1, POOL_WINDOWS[1], jnp.where(group == 2, POOL_WINDOWS[2], POOL_WINDOWS[3])))
    pos = t_in_seq * MIX_ROWS + lax.broadcasted_iota(jnp.int32, (MIX_ROWS, D_POOL), 0)
    count = jnp.minimum(pos + 1, window).astype(F32)
    pooled = (win_sum / count - u).astype(BF16)
    y_pool = jnp.dot(pooled, wpool_ref[...], preferred_element_type=F32) * pscale_ref[...]
    pcarry[...] = u[MIX_ROWS - POOL_HALO:, :]

    cu = proj[:, C_CC:C_CH] * proj[:, C_CH:D_PROJ]
    cext = jnp.concatenate([ccarry[...], cu], axis=0)
    conv = (convw_ref[2:3, :] * cext + convw_ref[1:2, :] * pltpu.roll(cext, 1, axis=0)
            + convw_ref[0:1, :] * pltpu.roll(cext, 2, axis=0))
    y_conv = proj[:, C_CB:C_CC] * conv[CONV_HALO:]
    ccarry[...] = cu[MIX_ROWS - CONV_HALO:, :]

    ypc_ref[:, 0:D_POOL] = y_pool.astype(BF16)
    ypc_ref[:, D_POOL:D_POOL + D_CONV] = y_conv.astype(BF16)


def _mix_in(x, g, w, bf, sel, wpool, pscale, convw, layer):
    n = x.shape[0]
    rows = lambda width: pl.BlockSpec((MIX_ROWS, width), lambda i: (i, 0))
    return pl.pallas_call(
        _mix_in_kernel,
        grid=(n // MIX_ROWS,),
        in_specs=[
            rows(D_MODEL),
            _resident((1, D_MODEL), layer),
            _resident((D_MODEL, D_PROJ), layer),
            _resident((1, LANES), layer),
            pl.BlockSpec((SUBLANES, LANES), lambda i: (0, 0)),
            _resident((D_POOL, D_POOL), layer),
            _resident((1, D_POOL), layer),
            _resident((CONV_WIDTH, D_CONV), layer),
        ],
        out_specs=[rows(D_ATTN), rows(2 * D_ATTN), rows(D_ATTN), rows(D_POOL + D_CONV)],
        out_shape=[
            jax.ShapeDtypeStruct((n, D_ATTN), BF16),
            jax.ShapeDtypeStruct((n, 2 * D_ATTN), BF16),
            jax.ShapeDtypeStruct((n, D_ATTN), BF16),
            jax.ShapeDtypeStruct((n, D_POOL + D_CONV), BF16),
        ],
        scratch_shapes=[
            pltpu.VMEM((SUBLANES, LANES), F32),
            pltpu.VMEM((POOL_HALO, D_POOL), F32),
            pltpu.VMEM((CONV_HALO, D_CONV), F32),
        ],
        compiler_params=_compiler_params(1),
        name="mix_in",
    )(x, g, w, bf, sel, wpool, pscale, convw)


def _attn_kernel(q_ref, kx_ref, v_ref, o_ref, vt_ref):
    pair = pl.program_id(1)
    vt = jnp.transpose(v_ref[...].astype(F32)).astype(BF16)
    ones = jnp.ones((ONES_ROWS, SEQ), BF16)
    for hd in range(2):
        vt_ref[hd, 0:HEAD_DIM, :] = vt[hd * HEAD_DIM:(hd + 1) * HEAD_DIM, :]
        vt_ref[hd, HEAD_DIM:HEAD_DIM + ONES_ROWS, :] = ones

    row = lax.broadcasted_iota(jnp.int32, (LANES, ATT_BLOCK), 0)
    krow = lax.broadcasted_iota(jnp.int32, (ATT_BLOCK, 2 * ATT_BLOCK), 0)
    qcol = lax.broadcasted_iota(jnp.int32, (ATT_BLOCK, 2 * ATT_BLOCK), 1)
    causal = krow <= jnp.where(qcol >= ATT_BLOCK, qcol - ATT_BLOCK, qcol)
    zeros_half = jnp.zeros((HEAD_DIM, ATT_BLOCK), BF16)
    picks = []
    for hd in range(2):
        gate_row = GATE_SPLIT * (2 * pair + hd)
        picks.append(jnp.where((row >= gate_row) & (row < gate_row + GATE_SPLIT), 1.0, 0.0).astype(BF16))

    def scores(qi):
        q0 = qi * ATT_BLOCK
        n_keys = q0 + ATT_BLOCK
        qt = jnp.transpose(q_ref[q0:n_keys, :].astype(F32)).astype(BF16)
        w = jnp.concatenate([
            jnp.concatenate([qt[0:HEAD_DIM, :], zeros_half, picks[0]], axis=0),
            jnp.concatenate([zeros_half, qt[HEAD_DIM:PAIR, :], picks[1]], axis=0)], axis=1)
        return jnp.dot(kx_ref[0:n_keys, :], w, preferred_element_type=F32)

    order = QUERY_BLOCK_ORDER
    ahead = [scores(qi) for qi in order[:SCORE_LOOKAHEAD]]
    for pos, qi in enumerate(order):
        q0 = qi * ATT_BLOCK
        n_keys = q0 + ATT_BLOCK
        st = ahead.pop(0)
        if pos + SCORE_LOOKAHEAD < len(order):
            ahead.append(scores(order[pos + SCORE_LOOKAHEAD]))
        diag = jnp.where(causal, st[q0:, :], MASKED)
        m = jnp.max(diag, axis=0, keepdims=True)
        if qi > 0:
            m = jnp.maximum(m, jnp.max(st[:q0, :], axis=0, keepdims=True))
            pt = jnp.concatenate([jnp.exp2(st[:q0, :] - m), jnp.exp2(diag - m)], axis=0)
        else:
            pt = jnp.exp2(diag - m)
        pt = pt.astype(BF16)
        o_heads = []
        for hd in range(2):
            acc = jnp.dot(vt_ref[hd, :, 0:n_keys], pt[:, hd * ATT_BLOCK:(hd + 1) * ATT_BLOCK],
                          preferred_element_type=F32)
            o_heads.append(acc[0:HEAD_DIM, :] / acc[HEAD_DIM:HEAD_DIM + 1, :])
        ot = jnp.concatenate(o_heads, axis=0)
        o_ref[q0:n_keys, :] = jnp.transpose(ot).astype(BF16)


def _attn(q, kx, v):
    n = q.shape[0]
    batch = n // SEQ
    return pl.pallas_call(
        _attn_kernel,
        grid=(batch, N_PAIRS),
        in_specs=[
            pl.BlockSpec((SEQ, PAIR), lambda b, p: (b, p)),
            pl.BlockSpec((SEQ, 2 * PAIR), lambda b, p: (b, p)),
            pl.BlockSpec((SEQ, PAIR), lambda b, p: (b, p)),
        ],
        out_specs=pl.BlockSpec((SEQ, PAIR), lambda b, p: (b, p)),
        out_shape=jax.ShapeDtypeStruct((n, D_ATTN), BF16),
        scratch_shapes=[pltpu.VMEM((2, HEAD_DIM + ONES_ROWS, SEQ), BF16)],
        compiler_params=_compiler_params(2),
        name="attn",
    )(q, kx, v)


def _mix_out_kernel(x_ref, ya_ref, ypc_ref, w_ref, o_ref):
    y = jnp.dot(ya_ref[...], w_ref[0:D_ATTN, :], preferred_element_type=F32)
    y = y + jnp.dot(ypc_ref[...], w_ref[D_ATTN:, :], preferred_element_type=F32)
    o_ref[...] = x_ref[...] + y


def _mix_out(x, ya, ypc, w, layer):
    n = x.shape[0]
    rows = lambda width: pl.BlockSpec((OUT_ROWS, width), lambda i: (i, 0))
    return pl.pallas_call(
        _mix_out_kernel,
        grid=(n // OUT_ROWS,),
        in_specs=[rows(D_MODEL), rows(D_ATTN), rows(D_POOL + D_CONV),
                  _resident((D_MODEL, D_MODEL), layer)],
        out_specs=rows(D_MODEL),
        out_shape=jax.ShapeDtypeStruct((n, D_MODEL), F32),
        compiler_params=_compiler_params(1),
        name="mix_out",
    )(x, ya, ypc, w)


def _norm_kernel(x_ref, g_ref, o_ref):
    o_ref[...] = _rmsnorm(x_ref[...], g_ref[...])


def _final_norm(x, g):
    n = x.shape[0]
    return pl.pallas_call(
        _norm_kernel,
        grid=(n // NORM_ROWS,),
        in_specs=[pl.BlockSpec((NORM_ROWS, D_MODEL), lambda i: (i, 0)),
                  pl.BlockSpec((1, D_MODEL), lambda i: (0, 0))],
        out_specs=pl.BlockSpec((NORM_ROWS, D_MODEL), lambda i: (i, 0)),
        out_shape=jax.ShapeDtypeStruct((n, D_MODEL), F32),
        compiler_params=_compiler_params(1),
        name="final_norm",
    )(x, g)


def _gate_lanes(a):
    rep = jnp.repeat(a, GATE_SPLIT, axis=-1)
    pad = [(0, 0)] * (a.ndim - 1) + [(0, LANES - GATE_SPLIT * N_HEADS)]
    return jnp.pad(rep, pad)


def _split_selector():
    sel = np.zeros((SUBLANES, LANES), np.float32)
    for lane in range(GATE_SPLIT * N_HEADS):
        sel[lane % GATE_SPLIT, lane] = 1.0
    return jnp.asarray(sel)


def _mix_in_weight(w_mix_in):
    o_f = 3 * D_ATTN
    o_pool = o_f + N_HEADS
    o_conv = o_pool + D_POOL
    qkv = w_mix_in[..., :o_f]
    gates = _gate_lanes(w_mix_in[..., o_f:o_pool])
    rest = w_mix_in[..., o_pool:]
    assert rest.shape[-1] == D_POOL + 3 * D_CONV and o_conv + 3 * D_CONV == D_IN
    return jnp.concatenate([qkv, gates, rest], axis=-1).astype(BF16)


def _pool_weight(w_pool):
    depth, groups = w_pool.shape[:2]
    eye = jnp.eye(groups, dtype=w_pool.dtype)
    bd = jnp.einsum("lgij,gh->lgihj", w_pool, eye)
    return bd.reshape(depth, D_POOL, D_POOL).astype(BF16)


def kernel(x, norm_ffn1, w_ffn1_in, w_ffn1_out, norm_mix, w_mix_in, b_forget, w_pool, pool_scale,
           conv_w, w_mix_out, norm_ffn2, w_ffn2_in, w_ffn2_out, norm_final):
    batch, seq, d_model = x.shape
    depth = norm_ffn1.shape[0]
    assert (seq, d_model) == (SEQ, D_MODEL) and w_mix_in.shape[-1] == D_IN
    assert w_ffn1_in.shape[1:] == (D_MODEL, 2 * D_FF)

    row3 = lambda a: a.reshape(depth, 1, a.shape[-1])
    g1, gm, g2 = row3(norm_ffn1), row3(norm_mix), row3(norm_ffn2)
    w1i, w1o = w_ffn1_in.astype(BF16), w_ffn1_out.astype(BF16)
    w2i, w2o = w_ffn2_in.astype(BF16), w_ffn2_out.astype(BF16)
    wmi = _mix_in_weight(w_mix_in)
    wmo = w_mix_out.astype(BF16)
    bf = row3(_gate_lanes(b_forget))
    sel = _split_selector()
    wpool = _pool_weight(w_pool)
    pscale = row3(pool_scale)

    h = x.reshape(batch * seq, d_model)
    for layer in range(depth):
        h = _ffn(h, g1, w1i, w1o, layer)
        q, kx, v, ypc = _mix_in(h, gm, wmi, bf, sel, wpool, pscale, conv_w, layer)
        ya = _attn(q, kx, v)
        h = _mix_out(h, ya, ypc, wmo, layer)
        h = _ffn(h, g2, w2i, w2o, layer)
    out = _final_norm(h, norm_final.reshape(1, d_model))
    return out.reshape(batch, seq, d_model)
```

```python
import functools

import numpy as np
import jax
import jax.numpy as jnp
from jax import lax
from jax.experimental import pallas as pl
from jax.experimental.pallas import tpu as pltpu

F32 = jnp.float32
BF16 = jnp.bfloat16

D_MODEL = 1024
SEQ = 2048
HEAD_DIM = 64
N_HEADS = 8
D_ATTN = N_HEADS * HEAD_DIM
D_POOL = 256
D_CONV = 256
POOL_WINDOWS = (2, 4, 8, 16)
POOL_GROUP_DIM = D_POOL // len(POOL_WINDOWS)
CONV_WIDTH = 3
D_FF = 2816
D_IN = 3 * D_ATTN + N_HEADS + D_POOL + 3 * D_CONV
RMS_EPS = 1e-6
ATTN_SCALE = HEAD_DIM ** -0.5
LOG2_E = 1.4426950408889634

LANES = 128
SUBLANES = 8
VMEM_LIMIT_BYTES = 56 * 1024 * 1024

FFN_ROWS = 1024
FFN_GROUP = 256
FF_CHUNKS = ((0, 1024), (1024, 2048), (2048, D_FF))
MIX_ROWS = 1024
MIX_GROUP = 256
ATT_BLOCK = 256
QUERY_BLOCK_ORDER = tuple(range(SEQ // ATT_BLOCK))
SCORE_LOOKAHEAD = 2
PAIR = 2 * HEAD_DIM
N_PAIRS = N_HEADS // 2
PAIRS_PER_STEP = 1
GATE_SPLIT = 3
POOL_HALO = max(POOL_WINDOWS)
CONV_HALO = SUBLANES
ONES_ROWS = 16
MASKED = -1e30

C_Q = 0
C_K = C_Q + D_ATTN
C_V = C_K + D_ATTN
C_F = C_V + D_ATTN
C_POOL = C_F + LANES
C_CB = C_POOL + D_POOL
C_CC = C_CB + D_CONV
C_CH = C_CC + D_CONV
D_PROJ = C_CH + D_CONV


def _rmsnorm(x, g):
    return x * lax.rsqrt(jnp.mean(x * x, axis=-1, keepdims=True) + RMS_EPS) * g


def _compiler_params(n_axes, flags=None):
    return pltpu.CompilerParams(
        dimension_semantics=("arbitrary",) * n_axes,
        vmem_limit_bytes=VMEM_LIMIT_BYTES,
        flags=flags,
    )


def _resident(shape, layer):
    nd = len(shape)
    return pl.BlockSpec((None,) + tuple(shape), lambda *_: (layer,) + (0,) * nd,
                        pipeline_mode=pl.Buffered(1))


def _swiglu_residual(x, g_ref, win_ref, wout_ref):
    h = _rmsnorm(x, g_ref[...]).astype(BF16)
    acc = None
    for c0, c1 in FF_CHUNKS:
        gate = jnp.dot(h, win_ref[:, c0:c1], preferred_element_type=F32)
        up = jnp.dot(h, win_ref[:, D_FF + c0:D_FF + c1], preferred_element_type=F32)
        act = (gate * jax.nn.sigmoid(gate) * up).astype(BF16)
        part = jnp.dot(act, wout_ref[c0:c1, :], preferred_element_type=F32)
        acc = part if acc is None else acc + part
    return x + 0.5 * acc


def _ffn_kernel(*refs, mixer, final):
    refs = list(refs)
    x_ref = refs.pop(0)
    if mixer:
        ya_ref, ypc_ref, wmix_ref = refs.pop(0), refs.pop(0), refs.pop(0)
    g_ref, win_ref, wout_ref = refs.pop(0), refs.pop(0), refs.pop(0)
    gfin_ref = refs.pop(0) if final else None
    (o_ref,) = refs

    groups = [slice(r0, r0 + FFN_GROUP) for r0 in range(0, FFN_ROWS, FFN_GROUP)]
    xs = []
    for rows in groups:
        x = x_ref[rows, :]
        if mixer:
            x = x + jnp.dot(ya_ref[rows, :], wmix_ref[0:D_ATTN, :], preferred_element_type=F32)
            x = x + jnp.dot(ypc_ref[rows, :], wmix_ref[D_ATTN:, :], preferred_element_type=F32)
        xs.append(x)
    for rows, x in zip(groups, xs):
        y = _swiglu_residual(x, g_ref, win_ref, wout_ref)
        o_ref[rows, :] = _rmsnorm(y, gfin_ref[...]) if final else y


def _ffn(x, g, w_in, w_out, layer, mixer=None, final_gain=None):
    n = x.shape[0]
    rows = lambda width: pl.BlockSpec((FFN_ROWS, width), lambda i: (i, 0))
    args, specs = [x], [rows(D_MODEL)]
    if mixer is not None:
        ya, ypc, w_mix = mixer
        args += [ya, ypc, w_mix]
        specs += [rows(D_ATTN), rows(D_POOL + D_CONV), _resident((D_MODEL, D_MODEL), layer)]
    args += [g, w_in, w_out]
    specs += [_resident((1, D_MODEL), layer), _resident((D_MODEL, 2 * D_FF), layer),
              _resident((D_FF, D_MODEL), layer)]
    if final_gain is not None:
        args.append(final_gain)
        specs.append(pl.BlockSpec((1, D_MODEL), lambda i: (0, 0)))
    return pl.pallas_call(
        functools.partial(_ffn_kernel, mixer=mixer is not None, final=final_gain is not None),
        grid=(n // FFN_ROWS,),
        in_specs=specs,
        out_specs=rows(D_MODEL),
        out_shape=jax.ShapeDtypeStruct((n, D_MODEL), F32),
        compiler_params=_compiler_params(1),
        name="ffn",
    )(*args)


def _mix_in_kernel(x_ref, g_ref, w_ref, bf_ref, sel_ref, wpool_ref, pscale_ref, convw_ref,
                   q_ref, kx_ref, v_ref, ypc_ref, dcarry, pcarry, ccarry):
    steps_per_seq = SEQ // MIX_ROWS
    step_in_seq = lax.rem(pl.program_id(0), steps_per_seq)

    @pl.when(step_in_seq == 0)
    def _():
        dcarry[...] = jnp.zeros_like(dcarry)
        pcarry[...] = jnp.zeros_like(pcarry)
        ccarry[...] = jnp.zeros_like(ccarry)

    groups = [slice(r0, r0 + MIX_GROUP) for r0 in range(0, MIX_ROWS, MIX_GROUP)]
    projs = []
    for rows in groups:
        xn = _rmsnorm(x_ref[rows, :], g_ref[...]).astype(BF16)
        projs.append(jnp.dot(xn, w_ref[...], preferred_element_type=F32))

    row_id = lax.broadcasted_iota(jnp.int32, (MIX_GROUP, LANES), 0)
    lane = lax.broadcasted_iota(jnp.int32, (MIX_GROUP, D_POOL), 1)
    group = lax.shift_right_logical(lane, POOL_GROUP_DIM.bit_length() - 1)
    window = jnp.where(group == 0, POOL_WINDOWS[0], jnp.where(
        group == 1, POOL_WINDOWS[1], jnp.where(group == 2, POOL_WINDOWS[2], POOL_WINDOWS[3])))
    pos_in_group = lax.broadcasted_iota(jnp.int32, (MIX_GROUP, D_POOL), 0)

    d_prev = dcarry[0:1, :]
    u_prev = pcarry[...]
    cu_prev = ccarry[...]
    for gi, (rows, proj) in enumerate(zip(groups, projs)):
        q_ref[rows, :] = (proj[:, C_Q:C_K] * (ATTN_SCALE * LOG2_E)).astype(BF16)
        v_ref[rows, :] = proj[:, C_V:C_F].astype(BF16)

        f = proj[:, C_F:C_POOL] + bf_ref[...]
        logf = jnp.minimum(f, 0.0) - jnp.log1p(jnp.exp(-jnp.abs(f)))
        csum = logf
        shift = 1
        while shift < MIX_GROUP:
            csum = csum + jnp.where(row_id >= shift, pltpu.roll(csum, shift, axis=0), 0.0)
            shift *= 2
        dcum = csum + d_prev
        d_prev = dcum[MIX_GROUP - 1:MIX_GROUP, :]
        d2 = dcum * LOG2_E
        hi = d2.astype(BF16).astype(F32)
        r1 = d2 - hi
        mid = r1.astype(BF16).astype(F32)
        lo = r1 - mid
        aug = -(hi * sel_ref[0:1, :] + mid * sel_ref[1:2, :] + lo * sel_ref[2:3, :])
        aug = aug.astype(BF16)
        for p in range(N_PAIRS):
            kx_ref[rows, 2 * PAIR * p:2 * PAIR * p + PAIR] = (
                proj[:, C_K + PAIR * p:C_K + PAIR * (p + 1)].astype(BF16))
            kx_ref[rows, 2 * PAIR * p + PAIR:2 * PAIR * (p + 1)] = aug

        u = proj[:, C_POOL:C_CB]
        ext = jnp.concatenate([u_prev, u], axis=0)
        s2 = ext + pltpu.roll(ext, 1, axis=0)
        s4 = s2 + pltpu.roll(s2, 2, axis=0)
        s8 = s4 + pltpu.roll(s4, 4, axis=0)
        s16 = s8 + pltpu.roll(s8, 8, axis=0)
        win_sum = jnp.where(group == 0, s2[POOL_HALO:], jnp.where(
            group == 1, s4[POOL_HALO:], jnp.where(group == 2, s8[POOL_HALO:], s16[POOL_HALO:])))
        pos = step_in_seq * MIX_ROWS + gi * MIX_GROUP + pos_in_group
        count = jnp.minimum(pos + 1, window).astype(F32)
        pooled = (win_sum / count - u).astype(BF16)
        y_pool = jnp.dot(pooled, wpool_ref[...], preferred_element_type=F32) * pscale_ref[...]
        u_prev = u[MIX_GROUP - POOL_HALO:, :]

        cu = proj[:, C_CC:C_CH] * proj[:, C_CH:D_PROJ]
        cext = jnp.concatenate([cu_prev, cu], axis=0)
        conv = (convw_ref[2:3, :] * cext + convw_ref[1:2, :] * pltpu.roll(cext, 1, axis=0)
                + convw_ref[0:1, :] * pltpu.roll(cext, 2, axis=0))
        y_conv = proj[:, C_CB:C_CC] * conv[CONV_HALO:]
        cu_prev = cu[MIX_GROUP - CONV_HALO:, :]

        ypc_ref[rows, 0:D_POOL] = y_pool.astype(BF16)
        ypc_ref[rows, D_POOL:D_POOL + D_CONV] = y_conv.astype(BF16)

    dcarry[...] = jnp.broadcast_to(d_prev, dcarry.shape)
    pcarry[...] = u_prev
    ccarry[...] = cu_prev


def _mix_in(x, g, w, bf, sel, wpool, pscale, convw, layer):
    n = x.shape[0]
    rows = lambda width: pl.BlockSpec((MIX_ROWS, width), lambda i: (i, 0))
    return pl.pallas_call(
        _mix_in_kernel,
        grid=(n // MIX_ROWS,),
        in_specs=[
            rows(D_MODEL),
            _resident((1, D_MODEL), layer),
            _resident((D_MODEL, D_PROJ), layer),
            _resident((1, LANES), layer),
            pl.BlockSpec((SUBLANES, LANES), lambda i: (0, 0)),
            _resident((D_POOL, D_POOL), layer),
            _resident((1, D_POOL), layer),
            _resident((CONV_WIDTH, D_CONV), layer),
        ],
        out_specs=[rows(D_ATTN), rows(2 * D_ATTN), rows(D_ATTN), rows(D_POOL + D_CONV)],
        out_shape=[
            jax.ShapeDtypeStruct((n, D_ATTN), BF16),
            jax.ShapeDtypeStruct((n, 2 * D_ATTN), BF16),
            jax.ShapeDtypeStruct((n, D_ATTN), BF16),
            jax.ShapeDtypeStruct((n, D_POOL + D_CONV), BF16),
        ],
        scratch_shapes=[
            pltpu.VMEM((SUBLANES, LANES), F32),
            pltpu.VMEM((POOL_HALO, D_POOL), F32),
            pltpu.VMEM((CONV_HALO, D_CONV), F32),
        ],
        compiler_params=_compiler_params(1),
        name="mix_in",
    )(x, g, w, bf, sel, wpool, pscale, convw)


def _attn_kernel(q_ref, kx_ref, v_ref, o_ref, vt_ref):
    ones = jnp.ones((ONES_ROWS, SEQ), BF16)
    row = lax.broadcasted_iota(jnp.int32, (LANES, ATT_BLOCK), 0)
    krow = lax.broadcasted_iota(jnp.int32, (ATT_BLOCK, 2 * ATT_BLOCK), 0)
    qcol = lax.broadcasted_iota(jnp.int32, (ATT_BLOCK, 2 * ATT_BLOCK), 1)
    causal = krow <= jnp.where(qcol >= ATT_BLOCK, qcol - ATT_BLOCK, qcol)
    zeros_half = jnp.zeros((HEAD_DIM, ATT_BLOCK), BF16)

    picks = {}
    for lp in range(PAIRS_PER_STEP):
        vt = jnp.transpose(v_ref[:, lp * PAIR:(lp + 1) * PAIR].astype(F32)).astype(BF16)
        for hd in range(2):
            vt_ref[2 * lp + hd, 0:HEAD_DIM, :] = vt[hd * HEAD_DIM:(hd + 1) * HEAD_DIM, :]
            vt_ref[2 * lp + hd, HEAD_DIM:HEAD_DIM + ONES_ROWS, :] = ones
            gate_row = GATE_SPLIT * (2 * (pl.program_id(1) * PAIRS_PER_STEP + lp) + hd)
            picks[lp, hd] = jnp.where((row >= gate_row) & (row < gate_row + GATE_SPLIT),
                                      1.0, 0.0).astype(BF16)

    def scores(item):
        lp, qi = item
        q0 = qi * ATT_BLOCK
        n_keys = q0 + ATT_BLOCK
        qt = jnp.transpose(q_ref[q0:n_keys, lp * PAIR:(lp + 1) * PAIR].astype(F32)).astype(BF16)
        w = jnp.concatenate([
            jnp.concatenate([qt[0:HEAD_DIM, :], zeros_half, picks[lp, 0]], axis=0),
            jnp.concatenate([zeros_half, qt[HEAD_DIM:PAIR, :], picks[lp, 1]], axis=0)], axis=1)
        return jnp.dot(kx_ref[0:n_keys, 2 * lp * PAIR:2 * (lp + 1) * PAIR], w,
                       preferred_element_type=F32)

    items = [(lp, qi) for lp in range(PAIRS_PER_STEP) for qi in QUERY_BLOCK_ORDER]
    ahead = [scores(item) for item in items[:SCORE_LOOKAHEAD]]
    for pos, (lp, qi) in enumerate(items):
        q0 = qi * ATT_BLOCK
        n_keys = q0 + ATT_BLOCK
        st = ahead.pop(0)
        if pos + SCORE_LOOKAHEAD < len(items):
            ahead.append(scores(items[pos + SCORE_LOOKAHEAD]))
        diag = jnp.where(causal, st[q0:, :], MASKED)
        m = jnp.max(diag, axis=0, keepdims=True)
        if qi > 0:
            m = jnp.maximum(m, jnp.max(st[:q0, :], axis=0, keepdims=True))
            pt = jnp.concatenate([jnp.exp2(st[:q0, :] - m), jnp.exp2(diag - m)], axis=0)
        else:
            pt = jnp.exp2(diag - m)
        pt = pt.astype(BF16)
        o_heads = []
        for hd in range(2):
            acc = jnp.dot(vt_ref[2 * lp + hd, :, 0:n_keys],
                          pt[:, hd * ATT_BLOCK:(hd + 1) * ATT_BLOCK], preferred_element_type=F32)
            o_heads.append(acc[0:HEAD_DIM, :] / acc[HEAD_DIM:HEAD_DIM + 1, :])
        ot = jnp.concatenate(o_heads, axis=0)
        o_ref[q0:n_keys, lp * PAIR:(lp + 1) * PAIR] = jnp.transpose(ot).astype(BF16)


def _attn(q, kx, v):
    n = q.shape[0]
    batch = n // SEQ
    width = PAIRS_PER_STEP * PAIR
    return pl.pallas_call(
        _attn_kernel,
        grid=(batch, N_PAIRS // PAIRS_PER_STEP),
        in_specs=[
            pl.BlockSpec((SEQ, width), lambda b, p: (b, p)),
            pl.BlockSpec((SEQ, 2 * width), lambda b, p: (b, p)),
            pl.BlockSpec((SEQ, width), lambda b, p: (b, p)),
        ],
        out_specs=pl.BlockSpec((SEQ, width), lambda b, p: (b, p)),
        out_shape=jax.ShapeDtypeStruct((n, D_ATTN), BF16),
        scratch_shapes=[pltpu.VMEM((2 * PAIRS_PER_STEP, HEAD_DIM + ONES_ROWS, SEQ), BF16)],
        compiler_params=_compiler_params(2),
        name="attn",
    )(q, kx, v)


def _gate_lanes(a):
    rep = jnp.repeat(a, GATE_SPLIT, axis=-1)
    pad = [(0, 0)] * (a.ndim - 1) + [(0, LANES - GATE_SPLIT * N_HEADS)]
    return jnp.pad(rep, pad)


def _split_selector():
    sel = np.zeros((SUBLANES, LANES), np.float32)
    for lane in range(GATE_SPLIT * N_HEADS):
        sel[lane % GATE_SPLIT, lane] = 1.0
    return jnp.asarray(sel)


def _mix_in_weight(w_mix_in):
    o_f = 3 * D_ATTN
    o_pool = o_f + N_HEADS
    o_conv = o_pool + D_POOL
    qkv = w_mix_in[..., :o_f]
    gates = _gate_lanes(w_mix_in[..., o_f:o_pool])
    rest = w_mix_in[..., o_pool:]
    assert rest.shape[-1] == D_POOL + 3 * D_CONV and o_conv + 3 * D_CONV == D_IN
    return jnp.concatenate([qkv, gates, rest], axis=-1).astype(BF16)


def _pool_weight(w_pool):
    depth, groups = w_pool.shape[:2]
    eye = jnp.eye(groups, dtype=w_pool.dtype)
    bd = jnp.einsum("lgij,gh->lgihj", w_pool, eye)
    return bd.reshape(depth, D_POOL, D_POOL).astype(BF16)


def kernel(x, norm_ffn1, w_ffn1_in, w_ffn1_out, norm_mix, w_mix_in, b_forget, w_pool, pool_scale,
           conv_w, w_mix_out, norm_ffn2, w_ffn2_in, w_ffn2_out, norm_final):
    batch, seq, d_model = x.shape
    depth = norm_ffn1.shape[0]
    assert (seq, d_model) == (SEQ, D_MODEL) and w_mix_in.shape[-1] == D_IN
    assert w_ffn1_in.shape[1:] == (D_MODEL, 2 * D_FF)

    row3 = lambda a: a.reshape(depth, 1, a.shape[-1])
    g1, gm, g2 = row3(norm_ffn1), row3(norm_mix), row3(norm_ffn2)
    w1i, w1o = w_ffn1_in.astype(BF16), w_ffn1_out.astype(BF16)
    w2i, w2o = w_ffn2_in.astype(BF16), w_ffn2_out.astype(BF16)
    wmi = _mix_in_weight(w_mix_in)
    wmo = w_mix_out.astype(BF16)
    bf = row3(_gate_lanes(b_forget))
    sel = _split_selector()
    wpool = _pool_weight(w_pool)
    pscale = row3(pool_scale)

    h = x.reshape(batch * seq, d_model)
    for layer in range(depth):
        h = _ffn(h, g1, w1i, w1o, layer)
        q, kx, v, ypc = _mix_in(h, gm, wmi, bf, sel, wpool, pscale, conv_w, layer)
        ya = _attn(q, kx, v)
        last = layer == depth - 1
        h = _ffn(h, g2, w2i, w2o, layer, mixer=(ya, ypc, wmo),
                 final_gain=norm_final.reshape(1, d_model) if last else None)
    return h.reshape(batch, seq, d_model)
```

```python
import functools

import numpy as np
import jax
import jax.numpy as jnp
from jax import lax
from jax.experimental import pallas as pl
from jax.experimental.pallas import tpu as pltpu

F32 = jnp.float32
BF16 = jnp.bfloat16

D_MODEL = 1024
SEQ = 2048
HEAD_DIM = 64
N_HEADS = 8
D_ATTN = N_HEADS * HEAD_DIM
D_POOL = 256
D_CONV = 256
POOL_WINDOWS = (2, 4, 8, 16)
POOL_GROUP_DIM = D_POOL // len(POOL_WINDOWS)
CONV_WIDTH = 3
D_FF = 2816
D_IN = 3 * D_ATTN + N_HEADS + D_POOL + 3 * D_CONV
RMS_EPS = 1e-6
ATTN_SCALE = HEAD_DIM ** -0.5
LOG2_E = 1.4426950408889634

LANES = 128
SUBLANES = 8
VMEM_LIMIT_BYTES = 56 * 1024 * 1024

FFN_ROWS = 1024
FFN_GROUP = 256
FF_CHUNKS = ((0, 1024), (1024, 2048), (2048, D_FF))
MIX_ROWS = 1024
MIX_GROUP = 256
ATT_BLOCK = 256
SCORE_LOOKAHEAD = 1
PAIR = 2 * HEAD_DIM
N_PAIRS = N_HEADS // 2
GATE_SPLIT = 3
POOL_HALO = max(POOL_WINDOWS)
CONV_HALO = SUBLANES
ONES_ROWS = 16
MASKED = -1e30

C_Q = 0
C_K = C_Q + D_ATTN
C_V = C_K + D_ATTN
C_F = C_V + D_ATTN
C_POOL = C_F + LANES
C_CB = C_POOL + D_POOL
C_CC = C_CB + D_CONV
C_CH = C_CC + D_CONV
D_PROJ = C_CH + D_CONV


def _rmsnorm(x, g):
    return x * lax.rsqrt(jnp.mean(x * x, axis=-1, keepdims=True) + RMS_EPS) * g


def _compiler_params(n_axes):
    return pltpu.CompilerParams(
        dimension_semantics=("arbitrary",) * n_axes,
        vmem_limit_bytes=VMEM_LIMIT_BYTES,
    )


def _resident(shape, layer):
    nd = len(shape)
    return pl.BlockSpec((None,) + tuple(shape), lambda *_: (layer,) + (0,) * nd,
                        pipeline_mode=pl.Buffered(1))


def _swiglu_residual(x, g_ref, win_ref, wout_ref):
    h = _rmsnorm(x, g_ref[...]).astype(BF16)
    acc = None
    for c0, c1 in FF_CHUNKS:
        gate = jnp.dot(h, win_ref[:, c0:c1], preferred_element_type=F32)
        up = jnp.dot(h, win_ref[:, D_FF + c0:D_FF + c1], preferred_element_type=F32)
        act = (gate * jax.nn.sigmoid(gate) * up).astype(BF16)
        part = jnp.dot(act, wout_ref[c0:c1, :], preferred_element_type=F32)
        acc = part if acc is None else acc + part
    return x + 0.5 * acc


def _ffn_kernel(*refs, mixer, final):
    refs = list(refs)
    x_ref = refs.pop(0)
    if mixer:
        ya_ref, ypc_ref, wmix_ref = refs.pop(0), refs.pop(0), refs.pop(0)
    g_ref, win_ref, wout_ref = refs.pop(0), refs.pop(0), refs.pop(0)
    gfin_ref = refs.pop(0) if final else None
    (o_ref,) = refs

    groups = [slice(r0, r0 + FFN_GROUP) for r0 in range(0, FFN_ROWS, FFN_GROUP)]
    xs = []
    for rows in groups:
        x = x_ref[rows, :]
        if mixer:
            x = x + jnp.dot(ya_ref[rows, :], wmix_ref[0:D_ATTN, :], preferred_element_type=F32)
            x = x + jnp.dot(ypc_ref[rows, :], wmix_ref[D_ATTN:, :], preferred_element_type=F32)
        xs.append(x)
    for rows, x in zip(groups, xs):
        y = _swiglu_residual(x, g_ref, win_ref, wout_ref)
        o_ref[rows, :] = _rmsnorm(y, gfin_ref[...]) if final else y


def _ffn(x, g, w_in, w_out, layer, mixer=None, final_gain=None):
    n = x.shape[0]
    rows = lambda width: pl.BlockSpec((FFN_ROWS, width), lambda i: (i, 0))
    args, specs = [x], [rows(D_MODEL)]
    if mixer is not None:
        ya, ypc, w_mix = mixer
        args += [ya, ypc, w_mix]
        specs += [rows(D_ATTN), rows(D_POOL + D_CONV), _resident((D_MODEL, D_MODEL), layer)]
    args += [g, w_in, w_out]
    specs += [_resident((1, D_MODEL), layer), _resident((D_MODEL, 2 * D_FF), layer),
              _resident((D_FF, D_MODEL), layer)]
    if final_gain is not None:
        args.append(final_gain)
        specs.append(pl.BlockSpec((1, D_MODEL), lambda i: (0, 0)))
    return pl.pallas_call(
        functools.partial(_ffn_kernel, mixer=mixer is not None, final=final_gain is not None),
        grid=(n // FFN_ROWS,),
        in_specs=specs,
        out_specs=rows(D_MODEL),
        out_shape=jax.ShapeDtypeStruct((n, D_MODEL), F32),
        compiler_params=_compiler_params(1),
        name="ffn",
    )(*args)


def _mix_in_kernel(x_ref, g_ref, w_ref, bf_ref, sel_ref, wpool_ref, pscale_ref, convw_ref,
                   q_ref, kx_ref, v_ref, ypc_ref, dcarry, pcarry, ccarry):
    steps_per_seq = SEQ // MIX_ROWS
    step_in_seq = lax.rem(pl.program_id(0), steps_per_seq)

    @pl.when(step_in_seq == 0)
    def _():
        dcarry[...] = jnp.zeros_like(dcarry)
        pcarry[...] = jnp.zeros_like(pcarry)
        ccarry[...] = jnp.zeros_like(ccarry)

    groups = [slice(r0, r0 + MIX_GROUP) for r0 in range(0, MIX_ROWS, MIX_GROUP)]
    projs = []
    for rows in groups:
        xn = _rmsnorm(x_ref[rows, :], g_ref[...]).astype(BF16)
        projs.append(jnp.dot(xn, w_ref[...], preferred_element_type=F32))

    row_id = lax.broadcasted_iota(jnp.int32, (MIX_GROUP, LANES), 0)
    lane = lax.broadcasted_iota(jnp.int32, (MIX_GROUP, D_POOL), 1)
    group = lax.shift_right_logical(lane, POOL_GROUP_DIM.bit_length() - 1)
    window = jnp.where(group == 0, POOL_WINDOWS[0], jnp.where(
        group == 1, POOL_WINDOWS[1], jnp.where(group == 2, POOL_WINDOWS[2], POOL_WINDOWS[3])))
    pos_in_group = lax.broadcasted_iota(jnp.int32, (MIX_GROUP, D_POOL), 0)

    d_prev = dcarry[0:1, :]
    u_prev = pcarry[...]
    cu_prev = ccarry[...]
    for gi, (rows, proj) in enumerate(zip(groups, projs)):
        q_ref[rows, :] = (proj[:, C_Q:C_K] * (ATTN_SCALE * LOG2_E)).astype(BF16)
        v_ref[rows, :] = proj[:, C_V:C_F].astype(BF16)

        f = proj[:, C_F:C_POOL] + bf_ref[...]
        logf = jnp.minimum(f, 0.0) - jnp.log1p(jnp.exp(-jnp.abs(f)))
        csum = logf
        shift = 1
        while shift < MIX_GROUP:
            csum = csum + jnp.where(row_id >= shift, pltpu.roll(csum, shift, axis=0), 0.0)
            shift *= 2
        dcum = csum + d_prev
        d_prev = dcum[MIX_GROUP - 1:MIX_GROUP, :]
        d2 = dcum * LOG2_E
        hi = d2.astype(BF16).astype(F32)
        r1 = d2 - hi
        mid = r1.astype(BF16).astype(F32)
        lo = r1 - mid
        aug = -(hi * sel_ref[0:1, :] + mid * sel_ref[1:2, :] + lo * sel_ref[2:3, :])
        aug = aug.astype(BF16)
        for p in range(N_PAIRS):
            kx_ref[rows, 2 * PAIR * p:2 * PAIR * p + PAIR] = (
                proj[:, C_K + PAIR * p:C_K + PAIR * (p + 1)].astype(BF16))
            kx_ref[rows, 2 * PAIR * p + PAIR:2 * PAIR * (p + 1)] = aug

        u = proj[:, C_POOL:C_CB]
        ext = jnp.concatenate([u_prev, u], axis=0)
        s2 = ext + pltpu.roll(ext, 1, axis=0)
        s4 = s2 + pltpu.roll(s2, 2, axis=0)
        s8 = s4 + pltpu.roll(s4, 4, axis=0)
        s16 = s8 + pltpu.roll(s8, 8, axis=0)
        win_sum = jnp.where(group == 0, s2[POOL_HALO:], jnp.where(
            group == 1, s4[POOL_HALO:], jnp.where(group == 2, s8[POOL_HALO:], s16[POOL_HALO:])))
        pos = step_in_seq * MIX_ROWS + gi * MIX_GROUP + pos_in_group
        count = jnp.minimum(pos + 1, window).astype(F32)
        pooled = (win_sum / count - u).astype(BF16)
        y_pool = jnp.dot(pooled, wpool_ref[...], preferred_element_type=F32) * pscale_ref[...]
        u_prev = u[MIX_GROUP - POOL_HALO:, :]

        cu = proj[:, C_CC:C_CH] * proj[:, C_CH:D_PROJ]
        cext = jnp.concatenate([cu_prev, cu], axis=0)
        conv = (convw_ref[2:3, :] * cext + convw_ref[1:2, :] * pltpu.roll(cext, 1, axis=0)
                + convw_ref[0:1, :] * pltpu.roll(cext, 2, axis=0))
        y_conv = proj[:, C_CB:C_CC] * conv[CONV_HALO:]
        cu_prev = cu[MIX_GROUP - CONV_HALO:, :]

        ypc_ref[rows, 0:D_POOL] = y_pool.astype(BF16)
        ypc_ref[rows, D_POOL:D_POOL + D_CONV] = y_conv.astype(BF16)

    dcarry[...] = jnp.broadcast_to(d_prev, dcarry.shape)
    pcarry[...] = u_prev
    ccarry[...] = cu_prev


def _mix_in(x, g, w, bf, sel, wpool, pscale, convw, layer):
    n = x.shape[0]
    rows = lambda width: pl.BlockSpec((MIX_ROWS, width), lambda i: (i, 0))
    return pl.pallas_call(
        _mix_in_kernel,
        grid=(n // MIX_ROWS,),
        in_specs=[
            rows(D_MODEL),
            _resident((1, D_MODEL), layer),
            _resident((D_MODEL, D_PROJ), layer),
            _resident((1, LANES), layer),
            pl.BlockSpec((SUBLANES, LANES), lambda i: (0, 0)),
            _resident((D_POOL, D_POOL), layer),
            _resident((1, D_POOL), layer),
            _resident((CONV_WIDTH, D_CONV), layer),
        ],
        out_specs=[rows(D_ATTN), rows(2 * D_ATTN), rows(D_ATTN), rows(D_POOL + D_CONV)],
        out_shape=[
            jax.ShapeDtypeStruct((n, D_ATTN), BF16),
            jax.ShapeDtypeStruct((n, 2 * D_ATTN), BF16),
            jax.ShapeDtypeStruct((n, D_ATTN), BF16),
            jax.ShapeDtypeStruct((n, D_POOL + D_CONV), BF16),
        ],
        scratch_shapes=[
            pltpu.VMEM((SUBLANES, LANES), F32),
            pltpu.VMEM((POOL_HALO, D_POOL), F32),
            pltpu.VMEM((CONV_HALO, D_CONV), F32),
        ],
        compiler_params=_compiler_params(1),
        name="mix_in",
    )(x, g, w, bf, sel, wpool, pscale, convw)


def _attn_kernel(q_ref, kx_ref, v_ref, o_ref, vt_ref):
    ones = jnp.ones((ONES_ROWS, SEQ), BF16)
    row = lax.broadcasted_iota(jnp.int32, (LANES, ATT_BLOCK), 0)
    krow = lax.broadcasted_iota(jnp.int32, (ATT_BLOCK, 2 * ATT_BLOCK), 0)
    qcol = lax.broadcasted_iota(jnp.int32, (ATT_BLOCK, 2 * ATT_BLOCK), 1)
    causal = krow <= jnp.where(qcol >= ATT_BLOCK, qcol - ATT_BLOCK, qcol)
    zeros_half = jnp.zeros((HEAD_DIM, ATT_BLOCK), BF16)

    vt = jnp.transpose(v_ref[...].astype(F32)).astype(BF16)
    picks = []
    for hd in range(2):
        vt_ref[hd, 0:HEAD_DIM, :] = vt[hd * HEAD_DIM:(hd + 1) * HEAD_DIM, :]
        vt_ref[hd, HEAD_DIM:HEAD_DIM + ONES_ROWS, :] = ones
        gate_row = GATE_SPLIT * (2 * pl.program_id(1) + hd)
        picks.append(jnp.where((row >= gate_row) & (row < gate_row + GATE_SPLIT),
                               1.0, 0.0).astype(BF16))

    def scores(qi):
        q0 = qi * ATT_BLOCK
        n_keys = q0 + ATT_BLOCK
        qt = jnp.transpose(q_ref[q0:n_keys, :].astype(F32)).astype(BF16)
        w = jnp.concatenate([
            jnp.concatenate([qt[0:HEAD_DIM, :], zeros_half, picks[0]], axis=0),
            jnp.concatenate([zeros_half, qt[HEAD_DIM:PAIR, :], picks[1]], axis=0)], axis=1)
        return jnp.dot(kx_ref[0:n_keys, :], w, preferred_element_type=F32)

    n_blocks = SEQ // ATT_BLOCK
    ahead = [scores(qi) for qi in range(SCORE_LOOKAHEAD)]
    for qi in range(n_blocks):
        q0 = qi * ATT_BLOCK
        n_keys = q0 + ATT_BLOCK
        st = ahead.pop(0)
        if qi + SCORE_LOOKAHEAD < n_blocks:
            ahead.append(scores(qi + SCORE_LOOKAHEAD))
        diag = jnp.where(causal, st[q0:, :], MASKED)
        m = jnp.max(diag, axis=0, keepdims=True)
        if qi > 0:
            m = jnp.maximum(m, jnp.max(st[:q0, :], axis=0, keepdims=True))
            pt = jnp.concatenate([jnp.exp2(st[:q0, :] - m), jnp.exp2(diag - m)], axis=0)
        else:
            pt = jnp.exp2(diag - m)
        pt = pt.astype(BF16)
        o_heads = []
        for hd in range(2):
            acc = jnp.dot(vt_ref[hd, :, 0:n_keys], pt[:, hd * ATT_BLOCK:(hd + 1) * ATT_BLOCK],
                          preferred_element_type=F32)
            o_heads.append(acc[0:HEAD_DIM, :] / acc[HEAD_DIM:HEAD_DIM + 1, :])
        ot = jnp.concatenate(o_heads, axis=0)
        o_ref[q0:n_keys, :] = jnp.transpose(ot).astype(BF16)


def _attn(q, kx, v):
    n = q.shape[0]
    batch = n // SEQ
    return pl.pallas_call(
        _attn_kernel,
        grid=(batch, N_PAIRS),
        in_specs=[
            pl.BlockSpec((SEQ, PAIR), lambda b, p: (b, p)),
            pl.BlockSpec((SEQ, 2 * PAIR), lambda b, p: (b, p)),
            pl.BlockSpec((SEQ, PAIR), lambda b, p: (b, p)),
        ],
        out_specs=pl.BlockSpec((SEQ, PAIR), lambda b, p: (b, p)),
        out_shape=jax.ShapeDtypeStruct((n, D_ATTN), BF16),
        scratch_shapes=[pltpu.VMEM((2, HEAD_DIM + ONES_ROWS, SEQ), BF16)],
        compiler_params=_compiler_params(2),
        name="attn",
    )(q, kx, v)


def _gate_lanes(a):
    rep = jnp.repeat(a, GATE_SPLIT, axis=-1)
    pad = [(0, 0)] * (a.ndim - 1) + [(0, LANES - GATE_SPLIT * N_HEADS)]
    return jnp.pad(rep, pad)


def _split_selector():
    sel = np.zeros((SUBLANES, LANES), np.float32)
    for lane in range(GATE_SPLIT * N_HEADS):
        sel[lane % GATE_SPLIT, lane] = 1.0
    return jnp.asarray(sel)


def _mix_in_weight(w_mix_in):
    o_f = 3 * D_ATTN
    o_pool = o_f + N_HEADS
    o_conv = o_pool + D_POOL
    qkv = w_mix_in[..., :o_f]
    gates = _gate_lanes(w_mix_in[..., o_f:o_pool])
    rest = w_mix_in[..., o_pool:]
    assert rest.shape[-1] == D_POOL + 3 * D_CONV and o_conv + 3 * D_CONV == D_IN
    return jnp.concatenate([qkv, gates, rest], axis=-1).astype(BF16)


def _pool_weight(w_pool):
    depth, groups = w_pool.shape[:2]
    eye = jnp.eye(groups, dtype=w_pool.dtype)
    bd = jnp.einsum("lgij,gh->lgihj", w_pool, eye)
    return bd.reshape(depth, D_POOL, D_POOL).astype(BF16)


def kernel(x, norm_ffn1, w_ffn1_in, w_ffn1_out, norm_mix, w_mix_in, b_forget, w_pool, pool_scale,
           conv_w, w_mix_out, norm_ffn2, w_ffn2_in, w_ffn2_out, norm_final):
    batch, seq, d_model = x.shape
    depth = norm_ffn1.shape[0]
    assert (seq, d_model) == (SEQ, D_MODEL) and w_mix_in.shape[-1] == D_IN
    assert w_ffn1_in.shape[1:] == (D_MODEL, 2 * D_FF)

    row3 = lambda a: a.reshape(depth, 1, a.shape[-1])
    g1, gm, g2 = row3(norm_ffn1), row3(norm_mix), row3(norm_ffn2)
    w1i, w1o = w_ffn1_in.astype(BF16), w_ffn1_out.astype(BF16)
    w2i, w2o = w_ffn2_in.astype(BF16), w_ffn2_out.astype(BF16)
    wmi = _mix_in_weight(w_mix_in)
    wmo = w_mix_out.astype(BF16)
    bf = row3(_gate_lanes(b_forget))
    sel = _split_selector()
    wpool = _pool_weight(w_pool)
    pscale = row3(pool_scale)

    h = x.reshape(batch * seq, d_model)
    for layer in range(depth):
        h = _ffn(h, g1, w1i, w1o, layer)
        q, kx, v, ypc = _mix_in(h, gm, wmi, bf, sel, wpool, pscale, conv_w, layer)
        ya = _attn(q, kx, v)
        last = layer == depth - 1
        h = _ffn(h, g2, w2i, w2o, layer, mixer=(ya, ypc, wmo),
                 final_gain=norm_final.reshape(1, d_model) if last else None)
    return h.reshape(batch, seq, d_model)
```
